```python
import math
import numpy as np
import jax
import jax.numpy as jnp
from jax import lax

D_MODEL = 4096
BATCH = 2
SEQ = 4096
DEPTH = 2

CTX_LEN = 256
GRID_W = 64
ROPE_THETA = 10000.0
EPS = 1e-6
Q_BLOCK = 128
F32 = jnp.float32

N_EVEN = (DEPTH + 1) // 2
N_ODD = DEPTH // 2
MIX_WIDTH = D_MODEL

A_HEAD_DIM = 128
A_WIDTH = MIX_WIDTH // 2
A_HEADS = A_WIDTH // A_HEAD_DIM
GLA_CHUNK = 64
B_VDIM = 128
B_HEADS = (MIX_WIDTH - A_WIDTH) // B_VDIM
B_NOPE = 128
B_ROPE = 64
B_Q_RANK = 768
B_KV_RANK = 512
EVEN_SPLITS = (A_WIDTH, A_WIDTH, A_WIDTH, A_WIDTH, A_WIDTH, B_Q_RANK, B_KV_RANK, B_ROPE)
EVEN_IN = 5 * A_WIDTH + B_Q_RANK + B_KV_RANK + B_ROPE
C_HEAD_DIM = 128
C_VDIM = 2 * C_HEAD_DIM
C_WIDTH = MIX_WIDTH // 2
C_HEADS = C_WIDTH // C_VDIM
D_WIDTH = MIX_WIDTH - C_WIDTH
D_BLOCKS = 16
D_BLOCK_DIM = D_WIDTH // D_BLOCKS
D_CONV = 4
RG_C = 8.0
ODD_SPLITS = (C_WIDTH, C_WIDTH, C_WIDTH, D_WIDTH, D_WIDTH)
ODD_IN = 3 * C_WIDTH + 2 * D_WIDTH
P_HEADS = 8
P_NKEYS = 128
P_EXPERTS = P_NKEYS * P_NKEYS
P_QDIM = 256
P_HALF = P_QDIM // 2
P_TOPK = 16
P_TOKEN_BLOCK = 64

kernel_name = 'hybrid_hgrn2_mla_diffattn_rglru_peer_dit'


def rms_norm(x, g):
    xf = x.astype(F32)
    y = xf * lax.rsqrt(jnp.mean(xf * xf, axis=-1, keepdims=True) + EPS)
    return (y * g.astype(F32)).astype(x.dtype)


def modulate(h, shift, scale):
    return h * (1.0 + scale) + shift


def split_cols(p, sizes):
    return jnp.split(p, np.cumsum(sizes)[:-1].tolist(), axis=-1)


def rope_tables(row, col, dim):
    n_f = dim // 4
    inv = ROPE_THETA ** (-jnp.arange(n_f, dtype=F32) / n_f)
    ang = jnp.concatenate([row[:, None].astype(F32) * inv, col[:, None].astype(F32) * inv], axis=-1)
    return jnp.cos(ang), jnp.sin(ang)


def apply_rope(x, cos, sin):
    half = x.shape[-1] // 2
    xf = x.astype(F32)
    x1, x2 = xf[..., :half], xf[..., half:]
    return jnp.concatenate([x1 * cos - x2 * sin, x2 * cos + x1 * sin], axis=-1).astype(x.dtype)


def attend(q, k, v, map_w, scale):
    B, H, M, Tq, dk = q.shape
    nb = Tq // Q_BLOCK
    kf, vf, wf = k.astype(F32), v.astype(F32), map_w.astype(F32)
    qb = jnp.moveaxis(q.astype(F32).reshape(B, H, M, nb, Q_BLOCK, dk), 3, 0)

    def block(qblk):
        s = jnp.einsum('bhmqd,bhmkd->bhmqk', qblk, kf) * scale
        p = jnp.einsum('bhmqk,m->bhqk', jax.nn.softmax(s, axis=-1), wf)
        return jnp.einsum('bhqk,bhkv->bhqv', p, vf)

    o = lax.map(block, qb)
    return jnp.moveaxis(o, 0, 2).reshape(B, H, Tq, v.shape[-1]).astype(v.dtype)


def gla_chunkwise(q, k, v, g, s0):
    B, H, T, K = q.shape
    V = v.shape[-1]
    n = T // GLA_CHUNK
    r = lambda t: t.reshape(B, H, n, GLA_CHUNK, t.shape[-1])
    q, k, v, g = r(q), r(k), r(v), r(g)
    b = jnp.cumsum(g, axis=3)
    b_mid = b[:, :, :, GLA_CHUNK // 2 - 1:GLA_CHUNK // 2]
    b_last = b[:, :, :, -1:]
    att = jnp.einsum('bhnck,bhnsk->bhncs', q * jnp.exp(b - b_mid), k * jnp.exp(b_mid - b))
    upto_t = jnp.tril(jnp.ones((GLA_CHUNK, GLA_CHUNK), dtype=bool))
    att = jnp.where(upto_t, att, 0.0)
    o = jnp.einsum('bhncs,bhnsv->bhncv', att, v)
    kv = jnp.einsum('bhnck,bhncv->bhnkv', k * jnp.exp(b_last - b), v)
    decay = jnp.exp(b_last[:, :, :, 0])

    def step(S, inp):
        dec, kv_c = inp
        return dec[..., None] * S + kv_c, S

    s_fin, s_start = lax.scan(step, s0, (jnp.moveaxis(decay, 2, 0), jnp.moveaxis(kv, 2, 0)))
    o = o + jnp.einsum('bhnck,nbhkv->bhncv', q * jnp.exp(b), s_start)
    return o.reshape(B, H, T, V), s_fin


def hgrn2_bidir(parts_c, parts_l, lb, onorm_g, need_ctx):
    def heads(t):
        B, T, _ = t.shape
        return t.reshape(B, T, A_HEADS, A_HEAD_DIM).transpose(0, 2, 1, 3)

    def prep(parts):
        q, f_f, f_b, i, _ = parts
        qh = heads(jax.nn.silu(q.astype(F32)) * A_HEAD_DIM ** -0.5)
        vh = heads(i.astype(F32))
        kg = []
        for d, fz in enumerate((f_f, f_b)):
            f = lb[d] + (1.0 - lb[d]) * jax.nn.sigmoid(fz.astype(F32))
            kg.append((heads(1.0 - f), heads(jnp.log(f))))
        return qh, vh, kg

    qc, vc, kgc = prep(parts_c)
    ql, vl, kgl = prep(parts_l)
    s0 = jnp.zeros((ql.shape[0], A_HEADS, A_HEAD_DIM, A_HEAD_DIM), F32)
    o_c, o_l = 0.0, 0.0
    for d in range(2):
        flip = (lambda t: jnp.flip(t, axis=2)) if d == 1 else (lambda t: t)
        oc_d, s_ctx = gla_chunkwise(flip(qc), flip(kgc[d][0]), flip(vc), flip(kgc[d][1]), s0)
        ol_d, _ = gla_chunkwise(flip(ql), flip(kgl[d][0]), flip(vl), flip(kgl[d][1]), s_ctx)
        o_l = o_l + flip(ol_d)
        if need_ctx:
            o_c = o_c + flip(oc_d)

    def readout(o, gate):
        B, H, T, V = o.shape
        o = rms_norm(o.transpose(0, 2, 1, 3), onorm_g).reshape(B, T, A_WIDTH)
        return o * jax.nn.silu(gate.astype(F32))

    out_l = readout(o_l, parts_l[4])
    out_c = readout(o_c, parts_c[4]) if need_ctx else None
    return out_c, out_l


def mla(parts_c, parts_l, rope_cs, cq_g, ckv_g, w_uq, w_ukv, qn_g, kn_g, qr_g, kr_g, need_ctx):
    cos, sin = rope_cs

    def project(parts, rotate):
        cq, ckv, kr = parts
        B, T, _ = cq.shape
        q = (rms_norm(cq, cq_g) @ w_uq).reshape(B, T, B_HEADS, B_NOPE + B_ROPE)
        kv = (rms_norm(ckv, ckv_g) @ w_ukv).reshape(B, T, B_HEADS, B_NOPE + B_VDIM)
        q_nope = rms_norm(q[..., :B_NOPE], qn_g)
        q_rope = rms_norm(q[..., B_NOPE:], qr_g)
        k_nope = rms_norm(kv[..., :B_NOPE], kn_g)
        v = kv[..., B_NOPE:]
        k_rope = rms_norm(kr, kr_g)
        if rotate:
            q_rope = apply_rope(q_rope, cos[:, None, :], sin[:, None, :])
            k_rope = apply_rope(k_rope, cos, sin)
        k_rope = jnp.broadcast_to(k_rope[:, :, None, :], (B, T, B_HEADS, B_ROPE))
        q = jnp.concatenate([q_nope, q_rope], -1).transpose(0, 2, 1, 3)[:, :, None]
        k = jnp.concatenate([k_nope, k_rope], -1).transpose(0, 2, 1, 3)[:, :, None]
        return q, k, v.transpose(0, 2, 1, 3)

    qc, kc, vc = project(parts_c, False)
    ql, kl, vl = project(parts_l, True)
    w1 = jnp.ones((1,), F32)
    scale = (B_NOPE + B_ROPE) ** -0.5

    def merge(o):
        B, H, T, V = o.shape
        return o.transpose(0, 2, 1, 3).reshape(B, T, H * V)

    out_l = merge(attend(ql, jnp.concatenate([kc, kl], axis=3), jnp.concatenate([vc, vl], axis=2), w1, scale))
    out_c = merge(attend(qc, kc, vc, w1, scale)) if need_ctx else None
    return out_c, out_l


def diff_attn(parts_c, parts_l, rope_cs, qn_g, kn_g, lam_p, onorm_g, lam_init, need_ctx):
    cos, sin = rope_cs

    def project(parts, rotate):
        q, k, v = parts
        B, T, _ = q.shape
        q = rms_norm(q.reshape(B, T, C_HEADS, 2, C_HEAD_DIM), qn_g)
        k = rms_norm(k.reshape(B, T, C_HEADS, 2, C_HEAD_DIM), kn_g)
        if rotate:
            cs, sn = cos[:, None, None, :], sin[:, None, None, :]
            q, k = apply_rope(q, cs, sn), apply_rope(k, cs, sn)
        v = v.reshape(B, T, C_HEADS, C_VDIM).transpose(0, 2, 1, 3)
        return q.transpose(0, 2, 3, 1, 4), k.transpose(0, 2, 3, 1, 4), v

    lp = lam_p.astype(F32)
    lam = jnp.exp(jnp.sum(lp[0] * lp[1])) - jnp.exp(jnp.sum(lp[2] * lp[3])) + lam_init
    map_w = jnp.stack([jnp.ones((), F32), -lam])
    scale = C_HEAD_DIM ** -0.5
    qc, kc, vc = project(parts_c, False)
    ql, kl, vl = project(parts_l, True)

    def readout(o):
        B, H, T, V = o.shape
        return (rms_norm(o.transpose(0, 2, 1, 3), onorm_g) * (1.0 - lam_init)).reshape(B, T, C_WIDTH)

    out_l = readout(attend(ql, jnp.concatenate([kc, kl], axis=3), jnp.concatenate([vc, vl], axis=2), map_w, scale))
    out_c = readout(attend(qc, kc, vc, map_w, scale)) if need_ctx else None
    return out_c, out_l


def dwconv(x, w, b):
    y = lax.conv_general_dilated(x, w[:, None, :].astype(x.dtype), window_strides=(1,),
                                 padding=[(D_CONV // 2, D_CONV - 1 - D_CONV // 2)],
                                 dimension_numbers=('NWC', 'WIO', 'NWC'), feature_group_count=x.shape[-1])
    return y + b.astype(x.dtype)


def rg_lru_coeffs(x, w, b, lam):
    B, T, _ = x.shape
    xf = x.astype(F32)
    z = jnp.einsum('btnc,gncd->gbtnd', xf.reshape(B, T, D_BLOCKS, D_BLOCK_DIM), w.astype(F32))
    gates = jax.nn.sigmoid(z.reshape(2, B, T, D_WIDTH) + b[:, None, None, :].astype(F32))
    log_a = -RG_C * gates[0] * jax.nn.softplus(-lam.astype(F32))
    return jnp.exp(log_a), jnp.sqrt(-jnp.expm1(2.0 * log_a)) * (gates[1] * xf)


def linear_scan(a, u, h0):
    u = u.at[:, 0].add(a[:, 0] * h0)
    comb = lambda l, r: (l[0] * r[0], r[0] * l[1] + r[1])
    return lax.associative_scan(comb, (a, u), axis=1)[1]


def rglru_bidir(gate_c, x_c, gate_l, x_l, conv_w, conv_b, w_gate, b_gate, lam, need_ctx):
    xc = dwconv(x_c, conv_w, conv_b)
    xl = dwconv(x_l, conv_w, conv_b)
    h0 = jnp.zeros((xl.shape[0], D_WIDTH), F32)
    h_c, h_l = 0.0, 0.0
    for d in range(2):
        flip = (lambda t: jnp.flip(t, axis=1)) if d == 1 else (lambda t: t)
        a_c, u_c = rg_lru_coeffs(flip(xc), w_gate[d], b_gate[d], lam[d])
        hc = linear_scan(a_c, u_c, h0)
        a_l, u_l = rg_lru_coeffs(flip(xl), w_gate[d], b_gate[d], lam[d])
        hl = linear_scan(a_l, u_l, hc[:, -1])
        h_l = h_l + flip(hl)
        if need_ctx:
            h_c = h_c + flip(hc)
    out_l = h_l * jax.nn.gelu(gate_l.astype(F32))
    out_c = h_c * jax.nn.gelu(gate_c.astype(F32)) if need_ctx else None
    return out_c, out_l


def peer(h, w_q, subkeys, u_tab, v_tab):
    N, D = h.shape
    q = (h @ w_q).reshape(N, P_HEADS, 2, P_HALF)
    s = jnp.einsum('nhpd,hpkd->nhpk', q, subkeys).astype(F32)
    s1, i1 = lax.top_k(s[:, :, 0], P_TOPK)
    s2, i2 = lax.top_k(s[:, :, 1], P_TOPK)
    cand_s = (s1[..., :, None] + s2[..., None, :]).reshape(N, P_HEADS, P_TOPK * P_TOPK)
    cand_i = (i1[..., :, None] * P_NKEYS + i2[..., None, :]).reshape(N, P_HEADS, P_TOPK * P_TOPK)
    top_s, pos = lax.top_k(cand_s, P_TOPK)
    idx = jnp.take_along_axis(cand_i, pos, axis=-1)
    gate = jax.nn.softmax(top_s, axis=-1)
    nb = N // P_TOKEN_BLOCK

    def block(args):
        hb, ib, gb = args
        u = jnp.take(u_tab, ib, axis=0)
        act = jax.nn.gelu(jnp.einsum('td,thkd->thk', hb, u).astype(F32), approximate=False) * gb
        return jnp.einsum('thk,thkd->td', act, jnp.take(v_tab, ib, axis=0).astype(F32))

    out = lax.map(block, (h.reshape(nb, P_TOKEN_BLOCK, D),
                          idx.reshape(nb, P_TOKEN_BLOCK, P_HEADS, P_TOPK),
                          gate.reshape(nb, P_TOKEN_BLOCK, P_HEADS, P_TOPK)))
    return out.reshape(N, D)


def setup_inputs(seed: int = 0) -> dict:
    key = jax.random.key(seed)
    ks = iter(jax.random.split(key, 48))
    nrm = lambda shape, s: jax.random.normal(next(ks), shape, F32) * s
    gain = lambda shape: 1.0 + 0.02 * jax.random.normal(next(ks), shape, F32)
    D = D_MODEL
    lam_u = jax.random.uniform(next(ks), (N_ODD, 2, D_WIDTH), F32, 0.9, 0.999)
    lam_s = lam_u ** (1.0 / RG_C)
    return {
        'x': nrm((BATCH, SEQ, D), 1.0),
        'c': nrm((BATCH, D), 1.0),
        'ctx': nrm((BATCH, CTX_LEN, D), 1.0),
        'c_ctx': nrm((D,), 1.0),
        'norm1_g': gain((DEPTH, D)),
        'norm2_g': gain((DEPTH, D)),
        'w_mod': nrm((DEPTH, D, 6 * D), 0.5 * D ** -0.5),
        'b_mod': nrm((DEPTH, 6 * D), 0.02),
        'e_w_in': nrm((N_EVEN, D, EVEN_IN), D ** -0.5),
        'e_w_out': nrm((N_EVEN, MIX_WIDTH, D), MIX_WIDTH ** -0.5),
        'a_lb_logits': nrm((2, N_EVEN + 1, A_WIDTH), 0.1),
        'a_onorm_g': gain((N_EVEN, A_HEAD_DIM)),
        'b_cq_g': gain((N_EVEN, B_Q_RANK)),
        'b_ckv_g': gain((N_EVEN, B_KV_RANK)),
        'b_w_uq': nrm((N_EVEN, B_Q_RANK, B_HEADS * (B_NOPE + B_ROPE)), B_Q_RANK ** -0.5),
        'b_w_ukv': nrm((N_EVEN, B_KV_RANK, B_HEADS * (B_NOPE + B_VDIM)), B_KV_RANK ** -0.5),
        'b_qn_g': gain((N_EVEN, B_NOPE)),
        'b_kn_g': gain((N_EVEN, B_NOPE)),
        'b_qr_g': gain((N_EVEN, B_ROPE)),
        'b_kr_g': gain((N_EVEN, B_ROPE)),
        'o_w_in': nrm((N_ODD, D, ODD_IN), D ** -0.5),
        'o_w_out': nrm((N_ODD, MIX_WIDTH, D), MIX_WIDTH ** -0.5),
        'c_qn_g': gain((N_ODD, C_HEAD_DIM)),
        'c_kn_g': gain((N_ODD, C_HEAD_DIM)),
        'c_lam': nrm((N_ODD, 4, C_HEAD_DIM), 0.1),
        'c_onorm_g': gain((N_ODD, C_VDIM)),
        'd_conv_w': nrm((N_ODD, D_CONV, D_WIDTH), D_CONV ** -0.5),
        'd_conv_b': nrm((N_ODD, D_WIDTH), 0.01),
        'd_w_gate': nrm((N_ODD, 2, 2, D_BLOCKS, D_BLOCK_DIM, D_BLOCK_DIM), D_BLOCK_DIM ** -0.5),
        'd_b_gate': nrm((N_ODD, 2, 2, D_WIDTH), 0.01),
        'd_lambda': jnp.log(lam_s) - jnp.log1p(-lam_s),
        'p_w_q': nrm((DEPTH, D, P_HEADS * P_QDIM), D ** -0.5),
        'p_subkeys': nrm((DEPTH, P_HEADS, 2, P_NKEYS, P_HALF), P_HALF ** -0.5),
        'p_u': nrm((DEPTH, P_EXPERTS, D), D ** -0.5),
        'p_v': nrm((DEPTH, P_EXPERTS, D), 0.5),
    }


def reference(x, c, ctx, c_ctx, norm1_g, norm2_g, w_mod, b_mod, e_w_in, e_w_out, a_lb_logits, a_onorm_g,
              b_cq_g, b_ckv_g, b_w_uq, b_w_ukv, b_qn_g, b_kn_g, b_qr_g, b_kr_g, o_w_in, o_w_out,
              c_qn_g, c_kn_g, c_lam, c_onorm_g, d_conv_w, d_conv_b, d_w_gate, d_b_gate, d_lambda,
              p_w_q, p_subkeys, p_u, p_v):
    B, T, D = x.shape
    Tc = ctx.shape[1]
    rows = T // GRID_W
    row = jnp.repeat(jnp.arange(rows), GRID_W)
    col = jnp.tile(jnp.arange(GRID_W), rows)
    rope_b = rope_tables(row, col, B_ROPE)
    rope_c = rope_tables(row, col, C_HEAD_DIM)
    lbs = jnp.cumsum(jax.nn.softmax(a_lb_logits.astype(F32), axis=1), axis=1)
    cond_l = jax.nn.silu(c)
    cond_c = jax.nn.silu(c_ctx)
    for l in range(DEPTH):
        j = l // 2
        need_ctx = l < DEPTH - 1
        mod_l = jnp.split((cond_l @ w_mod[l] + b_mod[l])[:, None, :], 6, axis=-1)
        mod_c = jnp.split((cond_c @ w_mod[l] + b_mod[l])[None, None, :], 6, axis=-1)
        hl = modulate(rms_norm(x, norm1_g[l]), mod_l[0], mod_l[1])
        hc = modulate(rms_norm(ctx, norm1_g[l]), mod_c[0], mod_c[1])
        if l % 2 == 0:
            pl = split_cols(hl @ e_w_in[j], EVEN_SPLITS)
            pc = split_cols(hc @ e_w_in[j], EVEN_SPLITS)
            g1_c, g1_l = hgrn2_bidir(pc[:5], pl[:5], lbs[:, j], a_onorm_g[j], need_ctx)
            g2_c, g2_l = mla(pc[5:], pl[5:], rope_b, b_cq_g[j], b_ckv_g[j], b_w_uq[j], b_w_ukv[j],
                             b_qn_g[j], b_kn_g[j], b_qr_g[j], b_kr_g[j], need_ctx)
            w_out = e_w_out[j]
        else:
            pl = split_cols(hl @ o_w_in[j], ODD_SPLITS)
            pc = split_cols(hc @ o_w_in[j], ODD_SPLITS)
            lam_init = 0.8 - 0.6 * math.exp(-0.3 * l)
            g1_c, g1_l = diff_attn(pc[:3], pl[:3], rope_c, c_qn_g[j], c_kn_g[j], c_lam[j], c_onorm_g[j],
                                   lam_init, need_ctx)
            g2_c, g2_l = rglru_bidir(pc[3], pc[4], pl[3], pl[4], d_conv_w[j], d_conv_b[j], d_w_gate[j],
                                     d_b_gate[j], d_lambda[j], need_ctx)
            w_out = o_w_out[j]
        x = x + (mod_l[2] * (jnp.concatenate([g1_l, g2_l], axis=-1) @ w_out)).astype(x.dtype)
        if need_ctx:
            ctx = ctx + (mod_c[2] * (jnp.concatenate([g1_c, g2_c], axis=-1) @ w_out)).astype(ctx.dtype)
        hl = modulate(rms_norm(x, norm2_g[l]), mod_l[3], mod_l[4]).reshape(B * T, D)
        if need_ctx:
            hc = modulate(rms_norm(ctx, norm2_g[l]), mod_c[3], mod_c[4]).reshape(B * Tc, D)
            y = peer(jnp.concatenate([hc, hl], axis=0), p_w_q[l], p_subkeys[l], p_u[l], p_v[l])
            ctx = ctx + (mod_c[5] * y[:B * Tc].reshape(B, Tc, D)).astype(ctx.dtype)
            y = y[B * Tc:]
        else:
            y = peer(hl, p_w_q[l], p_subkeys[l], p_u[l], p_v[l])
        x = x + (mod_l[5] * y.reshape(B, T, D)).astype(x.dtype)
    return x
```

```python
import functools
import math

import jax
import jax.numpy as jnp
from jax import lax
from jax.experimental import pallas as pl
from jax.experimental.pallas import tpu as pltpu

F32 = jnp.float32
BF16 = jnp.bfloat16

EPS = 1e-6
ROPE_THETA = 10000.0
GRID_W = 64
GLA_CHUNK = 64
RG_C = 8.0
D_CONV = 4
P_TOPK = 16

LANES = 128
SUBLANES = 8
V7X_VMEM_BYTES = 64 * 1024 * 1024
VMEM_LIMIT_BYTES = V7X_VMEM_BYTES - 8 * 1024 * 1024

NEG_INF = float("-inf")
HIGHEST = lax.Precision.HIGHEST


def _cparams(*sem):
    return pltpu.CompilerParams(dimension_semantics=sem, vmem_limit_bytes=VMEM_LIMIT_BYTES)


def _pick_block(dim, target, align):
    best = None
    for cand in range(align, min(dim, target) + 1, align):
        if dim % cand == 0:
            best = cand
    return dim if best is None else best


def _dot(a, b):
    return jnp.dot(a, b, preferred_element_type=F32)


def _dot_nt(a, b):
    return lax.dot_general(a, b, (((1,), (1,)), ((), ())), preferred_element_type=F32)


def _dot_tn(a, b):
    return lax.dot_general(a, b, (((0,), (0,)), ((), ())), preferred_element_type=F32)


def _rms(x, g, n):
    return x * lax.rsqrt(jnp.sum(x * x, axis=-1, keepdims=True) * (1.0 / n) + EPS) * g


def _silu(x):
    return x * jax.nn.sigmoid(x)


def _select_rows(rows, row0, n, segs):
    gidx = row0 + lax.broadcasted_iota(jnp.int32, (n, 1), 0)
    default = segs[0]
    out = jnp.broadcast_to(rows[default:default + 1, :], (n, rows.shape[1]))
    for start, stop, r in segs[1:]:
        inside = (gidx >= start) & (gidx < stop)
        out = jnp.where(inside, rows[r:r + 1, :], out)
    return out


def _mod_kernel(cond_ref, w_ref, b_ref, o_ref):
    a = _silu(cond_ref[...]).astype(BF16)
    o_ref[...] = _dot(a, w_ref[...].astype(BF16)) + b_ref[...]


def _mod_proj(cond, w, b):
    rows, d = cond.shape
    n = w.shape[1]
    bn = _pick_block(n, 1024, LANES)
    return pl.pallas_call(
        _mod_kernel,
        grid=(n // bn,),
        in_specs=[pl.BlockSpec((rows, d), lambda j: (0, 0)),
                  pl.BlockSpec((d, bn), lambda j: (0, j)),
                  pl.BlockSpec((1, bn), lambda j: (0, j))],
        out_specs=pl.BlockSpec((rows, bn), lambda j: (0, j)),
        out_shape=jax.ShapeDtypeStruct((rows, n), F32),
        compiler_params=_cparams("parallel"),
        name="mod_proj",
    )(cond, w, b.reshape(1, n))


def _norm_mod_kernel(x_ref, g_ref, shift_ref, scale_ref, o_ref, *, nb, t_lat):
    x = x_ref[...]
    rb, d = x.shape
    ridx = jnp.where(pl.program_id(1) * rb >= t_lat, nb, pl.program_id(0))
    shift = shift_ref[pl.ds(ridx, 1), :]
    scale = scale_ref[pl.ds(ridx, 1), :]
    y = _rms(x, g_ref[...], d)
    o_ref[...] = (y * (1.0 + scale) + shift).astype(o_ref.dtype)


def _norm_mod(x3, g, mod, which, nrows, t_lat):
    nb, _, d = x3.shape
    rb = _pick_block(math.gcd(nrows, t_lat), 256, SUBLANES)
    rows = mod.shape[0]
    return pl.pallas_call(
        functools.partial(_norm_mod_kernel, nb=nb, t_lat=t_lat),
        grid=(nb, nrows // rb),
        in_specs=[pl.BlockSpec((None, rb, d), lambda b, i: (b, i, 0)),
                  pl.BlockSpec((1, d), lambda b, i: (0, 0)),
                  pl.BlockSpec((rows, d), lambda b, i: (0, which)),
                  pl.BlockSpec((rows, d), lambda b, i: (0, which + 1))],
        out_specs=pl.BlockSpec((None, rb, d), lambda b, i: (b, i, 0)),
        out_shape=jax.ShapeDtypeStruct((nb, nrows, d), BF16),
        compiler_params=_cparams("parallel", "parallel"),
        name="norm_mod",
    )(x3, g.reshape(1, d), mod, mod)


def _mm_kernel(a_ref, w_ref, o_ref):
    o_ref[...] = _dot(a_ref[...], w_ref[...]).astype(o_ref.dtype)


def _mm(a, w, out_dtype):
    m, k = a.shape
    n = w.shape[1]
    bm = _pick_block(m, 1088, 16)
    bn = _pick_block(n, 1024, LANES)
    return pl.pallas_call(
        _mm_kernel,
        grid=(m // bm, n // bn),
        in_specs=[pl.BlockSpec((bm, k), lambda i, j: (i, 0)),
                  pl.BlockSpec((k, bn), lambda i, j: (0, j))],
        out_specs=pl.BlockSpec((bm, bn), lambda i, j: (i, j)),
        out_shape=jax.ShapeDtypeStruct((m, n), out_dtype),
        compiler_params=_cparams("parallel", "parallel"),
        name="matmul",
    )(a, w)


def _norm_mm_kernel(a_ref, g_ref, w_ref, o_ref):
    a = a_ref[...]
    y = _rms(a, g_ref[...], a.shape[1]).astype(BF16)
    o_ref[...] = _dot(y, w_ref[...]).astype(o_ref.dtype)


def _norm_mm(a, col_block, g, w, out_dtype):
    m = a.shape[0]
    k, n = w.shape
    bm = _pick_block(m, 1088, 16)
    bn = _pick_block(n, 1024, LANES)
    return pl.pallas_call(
        _norm_mm_kernel,
        grid=(m // bm, n // bn),
        in_specs=[pl.BlockSpec((bm, k), lambda i, j: (i, col_block)),
                  pl.BlockSpec((1, k), lambda i, j: (0, 0)),
                  pl.BlockSpec((k, bn), lambda i, j: (0, j))],
        out_specs=pl.BlockSpec((bm, bn), lambda i, j: (i, j)),
        out_shape=jax.ShapeDtypeStruct((m, n), out_dtype),
        compiler_params=_cparams("parallel", "parallel"),
        name="norm_matmul",
    )(a, g.reshape(1, k), w)


def _out_proj_kernel(a1_ref, a2_ref, w1_ref, w2_ref, x_ref, gate_ref, o_ref, *, segs):
    acc = _dot(a1_ref[...], w1_ref[...]) + _dot(a2_ref[...], w2_ref[...])
    bm = acc.shape[0]
    gate = _select_rows(gate_ref[...], pl.program_id(0) * bm, bm, segs)
    o_ref[...] = x_ref[...] + gate * acc


def _out_proj(a1, a2, w1, w2, x, mod, which, segs):
    m, d = x.shape
    k1, k2 = a1.shape[1], a2.shape[1]
    bm = _pick_block(m, 1088, 16)
    bn = _pick_block(d, 512, LANES)
    rows = mod.shape[0]
    nblk = d // bn
    return pl.pallas_call(
        functools.partial(_out_proj_kernel, segs=segs),
        grid=(m // bm, nblk),
        in_specs=[pl.BlockSpec((bm, k1), lambda i, j: (i, 0)),
                  pl.BlockSpec((bm, k2), lambda i, j: (i, 0)),
                  pl.BlockSpec((k1, bn), lambda i, j: (0, j)),
                  pl.BlockSpec((k2, bn), lambda i, j: (0, j)),
                  pl.BlockSpec((bm, bn), lambda i, j: (i, j)),
                  pl.BlockSpec((rows, bn), lambda i, j: (0, which * nblk + j))],
        out_specs=pl.BlockSpec((bm, bn), lambda i, j: (i, j)),
        out_shape=jax.ShapeDtypeStruct((m, d), F32),
        compiler_params=_cparams("parallel", "parallel"),
        name="out_proj",
    )(a1, a2, w1, w2, x, mod)


def _hgrn2_kernel(q_ref, ff_ref, fb_ref, i_ref, gate_ref, lbl_ref, on_ref, o_ref, acc_ref, *,
                  t_lat, t_ctx, nslots, slot):
    c = GLA_CHUNK
    hd = q_ref.shape[1]
    scale = hd ** -0.5
    ri = lax.broadcasted_iota(jnp.int32, (c, c), 0)
    ci = lax.broadcasted_iota(jnp.int32, (c, c), 1)
    n_ctx, n_lat = t_ctx // c, t_lat // c

    for d in range(2):
        logits = [lbl_ref[d * nslots + s:d * nslots + s + 1, :] for s in range(nslots)]
        mx = functools.reduce(jnp.maximum, logits)
        ex = [jnp.exp(l - mx) for l in logits]
        lb = functools.reduce(jnp.add, ex[:slot + 1]) / functools.reduce(jnp.add, ex)
        allowed = (ci <= ri) if d == 0 else (ci >= ri)
        cum = allowed.astype(F32)
        mid = c // 2 - 1 if d == 0 else c - c // 2
        last = c - 1 if d == 0 else 0
        fz_ref = ff_ref if d == 0 else fb_ref

        def chunk(row0, st, d=d, lb=lb, allowed=allowed, cum=cum, mid=mid, last=last, fz_ref=fz_ref):
            rows = pl.ds(row0, c)
            q = _silu(q_ref[rows, :]) * scale
            f = lb + (1.0 - lb) * jax.nn.sigmoid(fz_ref[rows, :])
            k = 1.0 - f
            v = i_ref[rows, :].astype(BF16)
            b = jnp.dot(cum, jnp.log(f), precision=HIGHEST, preferred_element_type=F32)
            b_mid = b[mid:mid + 1, :]
            b_last = b[last:last + 1, :]
            att = _dot_nt((q * jnp.exp(b - b_mid)).astype(BF16), (k * jnp.exp(b_mid - b)).astype(BF16))
            att = jnp.where(allowed, att, 0.0)
            o = _dot(att.astype(BF16), v)
            o = o + _dot_nt((q * jnp.exp(b)).astype(BF16), st.astype(BF16))
            kv_t = _dot_tn(v, (k * jnp.exp(b_last - b)).astype(BF16))
            st = st * jnp.exp(b_last) + kv_t
            if d == 0:
                acc_ref[rows, :] = o
            else:
                tot = acc_ref[rows, :] + o
                y = _rms(tot, on_ref[...], hd)
                o_ref[rows, :] = (y * _silu(gate_ref[rows, :])).astype(o_ref.dtype)
            return st

        def ctx_body(n, st, d=d, chunk=chunk):
            cidx = n if d == 0 else n_ctx - 1 - n
            return chunk(pl.multiple_of(t_lat + cidx * c, c), st)

        def lat_body(n, st, d=d, chunk=chunk):
            cidx = n if d == 0 else n_lat - 1 - n
            return chunk(pl.multiple_of(cidx * c, c), st)

        st = jnp.zeros((hd, hd), F32)
        st = lax.fori_loop(0, n_ctx, ctx_body, st)
        lax.fori_loop(0, n_lat, lat_body, st)


def _hgrn2(p3, lb_logits, onorm_g, slot, t_lat, t_ctx, width, hd):
    nb, t_tot, _ = p3.shape
    heads = width // hd
    nslots = lb_logits.shape[1]
    lbl = lb_logits.reshape(2 * nslots, width)
    part = lambda k: pl.BlockSpec((None, t_tot, hd), lambda b, h, k=k: (b, 0, k * heads + h))
    return pl.pallas_call(
        functools.partial(_hgrn2_kernel, t_lat=t_lat, t_ctx=t_ctx, nslots=nslots, slot=slot),
        grid=(nb, heads),
        in_specs=[part(0), part(1), part(2), part(3), part(4),
                  pl.BlockSpec((2 * nslots, hd), lambda b, h: (0, h)),
                  pl.BlockSpec((1, hd), lambda b, h: (0, 0))],
        out_specs=pl.BlockSpec((None, t_tot, hd), lambda b, h: (b, 0, h)),
        out_shape=jax.ShapeDtypeStruct((nb, t_tot, width), BF16),
        scratch_shapes=[pltpu.VMEM((t_tot, hd), F32)],
        compiler_params=_cparams("parallel", "parallel"),
        name="hgrn2",
    )(p3, p3, p3, p3, p3, lbl, onorm_g.reshape(1, hd))


def _rope_tables(t_lat, t_ctx, dim, lanes_used):
    rows = t_lat // GRID_W
    row = jnp.repeat(jnp.arange(rows), GRID_W)
    col = jnp.tile(jnp.arange(GRID_W), rows)
    n_f = dim // 4
    inv = ROPE_THETA ** (-jnp.arange(n_f, dtype=F32) / n_f)
    ang = jnp.concatenate([row[:, None].astype(F32) * inv, col[:, None].astype(F32) * inv], axis=-1)
    cos, sin = jnp.cos(ang), jnp.sin(ang)
    cos = jnp.concatenate([cos, jnp.ones((t_ctx, dim // 2), F32)], axis=0)
    sin = jnp.concatenate([sin, jnp.zeros((t_ctx, dim // 2), F32)], axis=0)
    reps = lanes_used // dim
    pad = jnp.zeros((t_lat + t_ctx, LANES - lanes_used), F32)
    cos_t = jnp.concatenate([cos, cos] * reps + [pad], axis=-1)
    sin_t = jnp.concatenate([-sin, sin] * reps + [pad], axis=-1)
    return cos_t, sin_t


def _mla_prep_kernel(q_ref, kv_ref, kr_ref, qn_ref, qr_ref, kn_ref, krg_ref, cos_ref, sin_ref,
                     qo_ref, ko_ref, vo_ref, *, heads, rope, scale):
    cosb, sinb = cos_ref[...], sin_ref[...]
    half = rope // 2
    first = lax.broadcasted_iota(jnp.int32, (1, LANES), 1) < half

    def rotate(x):
        partner = jnp.where(first, pltpu.roll(x, LANES - half, 1), pltpu.roll(x, half, 1))
        return x * cosb + partner * sinb

    kr = rotate(_rms(kr_ref[...], krg_ref[...], rope)).astype(BF16)
    for h in range(heads):
        lo, mid, hi = 2 * h * LANES, (2 * h + 1) * LANES, (2 * h + 2) * LANES
        qo_ref[:, lo:mid] = (_rms(q_ref[:, lo:mid], qn_ref[...], LANES) * scale).astype(BF16)
        qo_ref[:, mid:hi] = (rotate(_rms(q_ref[:, mid:hi], qr_ref[...], rope)) * scale).astype(BF16)
        ko_ref[:, lo:mid] = _rms(kv_ref[:, h * LANES:(h + 1) * LANES], kn_ref[...], LANES).astype(BF16)
        ko_ref[:, mid:hi] = kr
        vo_ref[:, h * LANES:(h + 1) * LANES] = kv_ref[:, (heads + h) * LANES:(heads + h + 1) * LANES].astype(BF16)


def _mla_prep(q, kv, pb, kr_block, gains, tables, heads, rope, scale, t_tot):
    m = q.shape[0]
    rb = _pick_block(t_tot, 256, 16)
    per_b = t_tot // rb
    row = lambda w: pl.BlockSpec((1, w), lambda i: (0, 0))
    tab = pl.BlockSpec((rb, LANES), lambda i: (i % per_b, 0))
    return pl.pallas_call(
        functools.partial(_mla_prep_kernel, heads=heads, rope=rope, scale=scale),
        grid=(m // rb,),
        in_specs=[pl.BlockSpec((rb, 2 * heads * LANES), lambda i: (i, 0)),
                  pl.BlockSpec((rb, 2 * heads * LANES), lambda i: (i, 0)),
                  pl.BlockSpec((rb, LANES), lambda i: (i, kr_block)),
                  row(LANES), row(LANES), row(LANES), row(LANES), tab, tab],
        out_specs=[pl.BlockSpec((rb, 2 * heads * LANES), lambda i: (i, 0)),
                   pl.BlockSpec((rb, 2 * heads * LANES), lambda i: (i, 0)),
                   pl.BlockSpec((rb, heads * LANES), lambda i: (i, 0))],
        out_shape=[jax.ShapeDtypeStruct((m, 2 * heads * LANES), BF16),
                   jax.ShapeDtypeStruct((m, 2 * heads * LANES), BF16),
                   jax.ShapeDtypeStruct((m, heads * LANES), BF16)],
        compiler_params=_cparams("parallel"),
        name="mla_prep",
    )(q, kv, pb, *gains, *tables)


def _ctx_key_mask(s, bq, t_lat):
    is_ctx = pl.program_id(2) * bq >= t_lat
    col = lax.broadcasted_iota(jnp.int32, (1, s.shape[1]), 1)
    return jnp.where(jnp.logical_and(is_ctx, col < t_lat), NEG_INF, s)


def _mla_attn_kernel(q_ref, k_ref, v_ref, o_ref, *, t_lat):
    bq = q_ref.shape[0]
    s = _ctx_key_mask(_dot_nt(q_ref[...], k_ref[...]), bq, t_lat)
    e = jnp.exp(s - jnp.max(s, axis=-1, keepdims=True))
    inv = 1.0 / jnp.sum(e, axis=-1, keepdims=True)
    o_ref[...] = (_dot(e.astype(BF16), v_ref[...]) * inv).astype(o_ref.dtype)


def _diff_attn_kernel(q_ref, k_ref, v_ref, lam_ref, on_ref, o_ref, *, t_lat, lam_init):
    bq = q_ref.shape[0]
    lp = lam_ref[...]
    lam = (jnp.exp(jnp.sum(lp[0:1] * lp[1:2], axis=-1, keepdims=True))
           - jnp.exp(jnp.sum(lp[2:3] * lp[3:4], axis=-1, keepdims=True)) + lam_init)
    p = None
    for mp in range(2):
        s = _dot_nt(q_ref[:, mp * LANES:(mp + 1) * LANES], k_ref[:, mp * LANES:(mp + 1) * LANES])
        s = _ctx_key_mask(s, bq, t_lat)
        e = jnp.exp(s - jnp.max(s, axis=-1, keepdims=True))
        inv = 1.0 / jnp.sum(e, axis=-1, keepdims=True)
        p = e * inv if mp == 0 else p - e * (lam * inv)
    o = _dot(p.astype(BF16), v_ref[...])
    o_ref[...] = (_rms(o, on_ref[...], o.shape[1]) * (1.0 - lam_init)).astype(o_ref.dtype)


def _attention(kernel_fn, q3, k3, v3, extra, heads, dv, n_q, t_lat, name):
    nb, t_tot, _ = q3.shape
    bq = _pick_block(math.gcd(n_q, t_lat), 256, 16)
    qw = 2 * LANES
    extra_specs = [pl.BlockSpec(e.shape, lambda b, h, i: (0, 0)) for e in extra]
    return pl.pallas_call(
        kernel_fn,
        grid=(nb, heads, n_q // bq),
        in_specs=[pl.BlockSpec((None, bq, qw), lambda b, h, i: (b, i, h)),
                  pl.BlockSpec((None, t_tot, qw), lambda b, h, i: (b, 0, h)),
                  pl.BlockSpec((None, t_tot, dv), lambda b, h, i: (b, 0, h))] + extra_specs,
        out_specs=pl.BlockSpec((None, bq, dv), lambda b, h, i: (b, i, h)),
        out_shape=jax.ShapeDtypeStruct((nb, n_q, heads * dv), BF16),
        compiler_params=_cparams("parallel", "parallel", "parallel"),
        name=name,
    )(q3, k3, v3, *extra)


def _diff_prep_kernel(q_ref, k_ref, v_ref, qn_ref, kn_ref, cos_ref, sin_ref, qo_ref, ko_ref, vo_ref, *,
                      tiles, scale):
    cosb, sinb = cos_ref[...], sin_ref[...]

    def rotate(x):
        return x * cosb + pltpu.roll(x, LANES // 2, 1) * sinb

    for t in range(tiles):
        sl = slice(t * LANES, (t + 1) * LANES)
        qo_ref[:, sl] = (rotate(_rms(q_ref[:, sl], qn_ref[...], LANES)) * scale).astype(BF16)
        ko_ref[:, sl] = rotate(_rms(k_ref[:, sl], kn_ref[...], LANES)).astype(BF16)
        vo_ref[:, sl] = v_ref[:, sl].astype(BF16)


def _diff_prep(p, width, gains, tables, scale, t_tot):
    m = p.shape[0]
    rb = _pick_block(t_tot, 256, 16)
    per_b = t_tot // rb
    row = pl.BlockSpec((1, LANES), lambda i: (0, 0))
    tab = pl.BlockSpec((rb, LANES), lambda i: (i % per_b, 0))
    blk = lambda k: pl.BlockSpec((rb, width), lambda i, k=k: (i, k))
    return pl.pallas_call(
        functools.partial(_diff_prep_kernel, tiles=width // LANES, scale=scale),
        grid=(m // rb,),
        in_specs=[blk(0), blk(1), blk(2), row, row, tab, tab],
        out_specs=[blk(0), blk(0), blk(0)],
        out_shape=[jax.ShapeDtypeStruct((m, width), BF16)] * 3,
        compiler_params=_cparams("parallel"),
        name="diff_prep",
    )(p, p, p, *gains, *tables)


def _log1p(z):
    w = 1.0 + z
    return jnp.where(w == 1.0, z, jnp.log(w) * (z / (w - 1.0)))


def _expm1(y):
    u = jnp.exp(y)
    return jnp.where(u == 1.0, y, jnp.where(u == 0.0, -1.0, (u - 1.0) * (y / jnp.log(u))))


def _gelu_tanh(x):
    return 0.5 * x * (1.0 + jnp.tanh(math.sqrt(2.0 / math.pi) * (x + 0.044715 * (x * x * x))))


def _rglru_kernel(gate_ref, x_ref, cw_ref, cb_ref, wg_ref, bg_ref, lam_ref, o_ref,
                  xp_ref, a_ref, u_ref, h_ref, *, t_lat, t_ctx, tile):
    t_tot = t_lat + t_ctx
    pad = SUBLANES
    bd = x_ref.shape[1]
    xp_ref[0:pad, :] = jnp.zeros((pad, bd), F32)
    xp_ref[pad + t_tot:2 * pad + t_tot, :] = jnp.zeros((pad, bd), F32)
    xp_ref[pad:pad + t_tot, :] = x_ref[...]
    cw, cb = cw_ref[...], cb_ref[...]
    sp = []
    for d in range(2):
        nl = -lam_ref[d:d + 1, :]
        sp.append(jnp.maximum(nl, 0.0) + _log1p(jnp.exp(-jnp.abs(nl))))

    for it in range(t_tot // tile):
        r0 = it * tile
        t = r0 + lax.broadcasted_iota(jnp.int32, (tile, 1), 0)
        seg_lo = jnp.where(t >= t_lat, t_lat, 0)
        seg_hi = jnp.where(t >= t_lat, t_tot, t_lat)
        xc = cb + cw[D_CONV // 2:D_CONV // 2 + 1, :] * xp_ref[pad + r0:pad + r0 + tile, :]
        for tap in range(D_CONV):
            off = tap - D_CONV // 2
            if off == 0:
                continue
            xs = xp_ref[pad + r0 + off:pad + r0 + off + tile, :]
            valid = jnp.logical_and(t + off >= seg_lo, t + off < seg_hi)
            xc = xc + cw[tap:tap + 1, :] * jnp.where(valid, xs, 0.0)
        xb = xc.astype(BF16)
        for d in range(2):
            r = jax.nn.sigmoid(_dot(xb, wg_ref[d, 0].astype(BF16)) + bg_ref[d, 0:1, :])
            i = jax.nn.sigmoid(_dot(xb, wg_ref[d, 1].astype(BF16)) + bg_ref[d, 1:2, :])
            log_a = -RG_C * r * sp[d]
            a_ref[d, r0:r0 + tile, :] = jnp.exp(log_a)
            u_ref[d, r0:r0 + tile, :] = jnp.sqrt(-_expm1(2.0 * log_a)) * (i * xc)

    sub = lax.broadcasted_iota(jnp.int32, (SUBLANES, 1), 0)

    def scan(d, start, ntiles, carry):
        rev = d == 1

        def body(n, carry):
            tidx = ntiles - 1 - n if rev else n
            rows = pl.ds(pl.multiple_of(start + tidx * SUBLANES, SUBLANES), SUBLANES)
            a, u = a_ref[d, rows, :], u_ref[d, rows, :]
            for s in (1, 2, 4):
                shift = SUBLANES - s if rev else s
                ok = (sub < SUBLANES - s) if rev else (sub >= s)
                u = u + jnp.where(ok, a * pltpu.roll(u, shift, 0), 0.0)
                a = jnp.where(ok, a * pltpu.roll(a, shift, 0), a)
            h = u + a * carry
            if rev:
                h_ref[rows, :] = h_ref[rows, :] + h
                return h[0:1, :]
            h_ref[rows, :] = h
            return h[SUBLANES - 1:SUBLANES, :]

        return lax.fori_loop(0, ntiles, body, carry)

    for d in range(2):
        carry = scan(d, t_lat, t_ctx // SUBLANES, jnp.zeros((1, bd), F32))
        scan(d, 0, t_lat // SUBLANES, carry)

    for it in range(t_tot // tile):
        rows = slice(it * tile, (it + 1) * tile)
        o_ref[rows, :] = (h_ref[rows, :] * _gelu_tanh(gate_ref[rows, :])).astype(o_ref.dtype)


def _rglru(p3, gate_col, x_col, conv_w, conv_b, w_gate, b_gate, lam, t_lat, t_ctx):
    nb, t_tot, _ = p3.shape
    nblk, bd = w_gate.shape[2], w_gate.shape[3]
    width = nblk * bd
    tile = _pick_block(t_tot, 256, SUBLANES)
    return pl.pallas_call(
        functools.partial(_rglru_kernel, t_lat=t_lat, t_ctx=t_ctx, tile=tile),
        grid=(nb, nblk),
        in_specs=[pl.BlockSpec((None, t_tot, bd), lambda b, j: (b, 0, gate_col // bd + j)),
                  pl.BlockSpec((None, t_tot, bd), lambda b, j: (b, 0, x_col // bd + j)),
                  pl.BlockSpec((D_CONV, bd), lambda b, j: (0, j)),
                  pl.BlockSpec((1, bd), lambda b, j: (0, j)),
                  pl.BlockSpec((2, 2, None, bd, bd), lambda b, j: (0, 0, j, 0, 0)),
                  pl.BlockSpec((2, 2, bd), lambda b, j: (0, 0, j)),
                  pl.BlockSpec((2, bd), lambda b, j: (0, j))],
        out_specs=pl.BlockSpec((None, t_tot, bd), lambda b, j: (b, 0, j)),
        out_shape=jax.ShapeDtypeStruct((nb, t_tot, width), BF16),
        scratch_shapes=[pltpu.VMEM((t_tot + 2 * SUBLANES, bd), F32),
                        pltpu.VMEM((2, t_tot, bd), F32),
                        pltpu.VMEM((2, t_tot, bd), F32),
                        pltpu.VMEM((t_tot, bd), F32)],
        compiler_params=_cparams("parallel", "parallel"),
        name="rglru",
    )(p3, p3, conv_w, conv_b.reshape(1, width), w_gate, b_gate, lam)


def _hyperbola_pairs():
    return [(a, b) for a in range(P_TOPK) for b in range(P_TOPK) if (a + 1) * (b + 1) <= P_TOPK]


def _peer_topk_kernel(q_ref, sk_ref, s1_ref, s2_ref, e1_ref, e2_ref, tau_ref, top_ref, *, heads):
    half = sk_ref.shape[3]
    for h in range(heads):
        for p in range(2):
            col = (2 * h + p) * half
            qs = q_ref[:, col:col + half].astype(BF16)
            s = _dot_nt(sk_ref[h, p].astype(BF16), qs)
            (s1_ref if p == 0 else s2_ref)[h] = s

            def extract(a, x, h=h, p=p):
                mx = jnp.max(x, axis=0, keepdims=True)
                top_ref[p, a, pl.ds(h, 1), :] = mx
                return jnp.where(x == mx, NEG_INF, x)

            lax.fori_loop(0, P_TOPK, extract, s)

    cands = [top_ref[0, a] + top_ref[1, b] for a, b in _hyperbola_pairs()]
    best = top_ref[0, 0] + top_ref[1, 0]
    z = jnp.zeros_like(best)
    cur = best
    for it in range(P_TOPK):
        cur = functools.reduce(jnp.maximum, cands)
        z = z + jnp.exp(cur - best)
        if it + 1 < P_TOPK:
            cands = [jnp.where(cd == cur, NEG_INF, cd) for cd in cands]
    tau_ref[...] = cur
    inv_z = 1.0 / z
    top1, top2 = top_ref[0, 0], top_ref[1, 0]
    for h in range(heads):
        e1_ref[h] = jnp.exp(s1_ref[h] - top1[h:h + 1, :])
        e2_ref[h] = jnp.exp(s2_ref[h] - top2[h:h + 1, :]) * inv_z[h:h + 1, :]


def _peer_topk(q, subkeys):
    m = q.shape[0]
    heads, _, nkeys, half = subkeys.shape
    tb = _pick_block(m, 512, LANES)
    stat = pl.BlockSpec((heads, nkeys, tb), lambda i: (0, 0, i))
    stat_shape = jax.ShapeDtypeStruct((heads, nkeys, m), F32)
    return pl.pallas_call(
        functools.partial(_peer_topk_kernel, heads=heads),
        grid=(m // tb,),
        in_specs=[pl.BlockSpec((tb, 2 * heads * half), lambda i: (i, 0)),
                  pl.BlockSpec((heads, 2, nkeys, half), lambda i: (0, 0, 0, 0))],
        out_specs=[stat, stat, stat, stat, pl.BlockSpec((heads, tb), lambda i: (0, i))],
        out_shape=[stat_shape] * 4 + [jax.ShapeDtypeStruct((heads, m), F32)],
        scratch_shapes=[pltpu.VMEM((2, P_TOPK, heads, tb), F32)],
        compiler_params=_cparams("parallel"),
        name="peer_topk",
    )(q, subkeys)


def _peer_main_kernel(h_ref, u_ref, v_ref, s1_ref, s2_ref, e1_ref, e2_ref, tau_ref, o_ref, *, heads, nkeys):
    j = pl.program_id(1)
    ec = u_ref.shape[0]
    per_step = ec // nkeys

    @pl.when(j == 0)
    def _():
        o_ref[...] = jnp.zeros_like(o_ref)

    s_t = _dot_nt(u_ref[...], h_ref[...])
    parts = []
    for ii in range(per_step):
        i1 = j * per_step + ii
        g = None
        for h in range(heads):
            s1row = s1_ref[h, pl.ds(i1, 1), :]
            e1row = e1_ref[h, pl.ds(i1, 1), :]
            chosen = (s2_ref[h] + s1row) >= tau_ref[h:h + 1, :]
            gh = jnp.where(chosen, e2_ref[h] * e1row, 0.0)
            g = gh if g is None else g + gh
        x = s_t[ii * nkeys:(ii + 1) * nkeys, :]
        act = 0.5 * x * (1.0 + lax.erf(x * math.sqrt(0.5)))
        parts.append((act * g).astype(BF16))
    a_t = parts[0] if per_step == 1 else jnp.concatenate(parts, axis=0)
    o_ref[...] += _dot_tn(a_t, v_ref[...])


def _peer_main(h, u, v, stats, nkeys):
    m, d = h.shape
    s1, s2, e1, e2, tau = stats
    heads = s1.shape[0]
    tb = _pick_block(m, 512, LANES)
    ec = 2 * nkeys
    stat = pl.BlockSpec((heads, nkeys, tb), lambda i, j: (0, 0, i))
    return pl.pallas_call(
        functools.partial(_peer_main_kernel, heads=heads, nkeys=nkeys),
        grid=(m // tb, u.shape[0] // ec),
        in_specs=[pl.BlockSpec((tb, d), lambda i, j: (i, 0)),
                  pl.BlockSpec((ec, d), lambda i, j: (j, 0)),
                  pl.BlockSpec((ec, d), lambda i, j: (j, 0)),
                  stat, stat, stat, stat,
                  pl.BlockSpec((heads, tb), lambda i, j: (0, i))],
        out_specs=pl.BlockSpec((tb, d), lambda i, j: (i, 0)),
        out_shape=jax.ShapeDtypeStruct((m, d), F32),
        compiler_params=_cparams("parallel", "arbitrary"),
        name="peer_main",
    )(h, u, v, s1, s2, e1, e2, tau)


def _resid_kernel(x_ref, y_ref, gate_ref, o_ref, *, nb, t_lat):
    rb = x_ref.shape[0]
    ridx = jnp.where(pl.program_id(1) * rb >= t_lat, nb, pl.program_id(0))
    o_ref[...] = x_ref[...] + gate_ref[pl.ds(ridx, 1), :] * y_ref[...]


def _resid(x3, y3, mod, which, t_lat):
    nb, _, d = x3.shape
    nrows = y3.shape[1]
    rb = _pick_block(math.gcd(nrows, t_lat), 256, SUBLANES)
    rows = mod.shape[0]
    blk = pl.BlockSpec((None, rb, d), lambda b, i: (b, i, 0))
    return pl.pallas_call(
        functools.partial(_resid_kernel, nb=nb, t_lat=t_lat),
        grid=(nb, nrows // rb),
        in_specs=[blk, blk, pl.BlockSpec((rows, d), lambda b, i: (0, which))],
        out_specs=blk,
        out_shape=jax.ShapeDtypeStruct((nb, nrows, d), F32),
        compiler_params=_cparams("parallel", "parallel"),
        name="peer_resid",
    )(x3, y3, mod)


def kernel(x, c, ctx, c_ctx, norm1_g, norm2_g, w_mod, b_mod, e_w_in, e_w_out, a_lb_logits, a_onorm_g, b_cq_g, b_ckv_g, b_w_uq, b_w_ukv, b_qn_g, b_kn_g, b_qr_g, b_kr_g, o_w_in, o_w_out, c_qn_g, c_kn_g, c_lam, c_onorm_g, d_conv_w, d_conv_b, d_w_gate, d_b_gate, d_lambda, p_w_q, p_subkeys, p_u, p_v):
    nb, t_lat, d = x.shape
    t_ctx = ctx.shape[1]
    t_tot = t_lat + t_ctx
    m = nb * t_tot
    depth = norm1_g.shape[0]

    a_hd = a_onorm_g.shape[1]
    a_width = a_lb_logits.shape[2]
    q_rank, kv_rank = b_cq_g.shape[1], b_ckv_g.shape[1]
    nope, rope = b_qn_g.shape[1], b_qr_g.shape[1]
    b_heads = b_w_uq.shape[2] // (nope + rope)
    vdim = b_w_ukv.shape[2] // b_heads - nope
    c_hd = c_qn_g.shape[1]
    c_vdim = c_onorm_g.shape[1]
    d_width = d_conv_w.shape[2]
    c_width = (o_w_in.shape[2] - 2 * d_width) // 3
    c_heads = c_width // c_vdim
    nkeys = p_subkeys.shape[3]
    assert nope == LANES and vdim == LANES and c_hd == LANES and c_vdim == 2 * LANES and a_hd == LANES
    assert 2 * rope == LANES and nkeys == LANES and q_rank % LANES == 0 and kv_rank % LANES == 0

    cond = jnp.concatenate([c, c_ctx[None, :], jnp.zeros((SUBLANES - nb - 1, d), F32)], axis=0)
    segs = (nb,) + tuple((b * t_tot, b * t_tot + t_lat, b) for b in range(nb))
    pad_row = lambda g: jnp.concatenate([g, jnp.zeros((LANES - g.shape[0],), F32)]).reshape(1, LANES)

    xs = jnp.concatenate([x, ctx], axis=1)
    out = None
    for l in range(depth):
        j = l // 2
        last = l == depth - 1
        mod = _mod_proj(cond, w_mod[l], b_mod[l])
        h1 = _norm_mod(xs, norm1_g[l], mod, 0, t_tot, t_lat).reshape(m, d)

        if l % 2 == 0:
            w_in = e_w_in[j]
            o_kr = 5 * a_width + q_rank + kv_rank
            w_a = w_in[:, :5 * a_width].astype(BF16)
            kr_pad = (-(q_rank + rope)) % kv_rank
            w_b = jnp.concatenate([w_in[:, 5 * a_width:5 * a_width + q_rank], w_in[:, o_kr:o_kr + rope],
                                   jnp.zeros((d, kr_pad), F32),
                                   w_in[:, 5 * a_width + q_rank:o_kr]], axis=1).astype(BF16)
            p_a = _mm(h1, w_a, F32)
            p_b = _mm(h1, w_b, F32)
            g1 = _hgrn2(p_a.reshape(nb, t_tot, 5 * a_width), a_lb_logits, a_onorm_g[j], j,
                        t_lat, t_ctx, a_width, a_hd).reshape(m, a_width)

            w_uq = b_w_uq[j].reshape(q_rank, b_heads, nope + rope)
            w_uq = jnp.pad(w_uq, ((0, 0), (0, 0), (0, 2 * LANES - nope - rope))).reshape(q_rank, -1).astype(BF16)
            w_ukv = b_w_ukv[j].reshape(kv_rank, b_heads, 2, LANES).transpose(0, 2, 1, 3)
            w_ukv = w_ukv.reshape(kv_rank, -1).astype(BF16)
            q = _norm_mm(p_b, 0, b_cq_g[j], w_uq, F32)
            kv = _norm_mm(p_b, (q_rank + rope + kr_pad) // kv_rank, b_ckv_g[j], w_ukv, F32)
            tables = _rope_tables(t_lat, t_ctx, rope, rope)
            gains = (b_qn_g[j].reshape(1, LANES), pad_row(b_qr_g[j]), b_kn_g[j].reshape(1, LANES),
                     pad_row(b_kr_g[j]))
            qc, kc, vc = _mla_prep(q, kv, p_b, q_rank // LANES, gains, tables, b_heads, rope,
                                   (nope + rope) ** -0.5, t_tot)
            r3 = lambda t: t.reshape(nb, t_tot, t.shape[1])
            g2 = _attention(functools.partial(_mla_attn_kernel, t_lat=t_lat), r3(qc), r3(kc), r3(vc), [],
                            b_heads, vdim, t_tot, t_lat, "mla_attn").reshape(m, b_heads * vdim)
            w_out = e_w_out[j].astype(BF16)
        else:
            p_o = _mm(h1, o_w_in[j].astype(BF16), F32)
            lam_init = 0.8 - 0.6 * math.exp(-0.3 * l)
            tables = _rope_tables(t_lat, t_ctx, c_hd, LANES)
            qd, kd, vd = _diff_prep(p_o, c_width, (c_qn_g[j].reshape(1, LANES), c_kn_g[j].reshape(1, LANES)),
                                    tables, c_hd ** -0.5, t_tot)
            r3 = lambda t: t.reshape(nb, t_tot, t.shape[1])
            g1 = _attention(functools.partial(_diff_attn_kernel, t_lat=t_lat, lam_init=lam_init),
                            r3(qd), r3(kd), r3(vd), [c_lam[j], c_onorm_g[j].reshape(1, c_vdim)],
                            c_heads, c_vdim, t_tot, t_lat, "diff_attn").reshape(m, c_width)
            g2 = _rglru(p_o.reshape(nb, t_tot, -1), 3 * c_width, 3 * c_width + d_width,
                        d_conv_w[j], d_conv_b[j], d_w_gate[j], d_b_gate[j], d_lambda[j],
                        t_lat, t_ctx).reshape(m, d_width)
            w_out = o_w_out[j].astype(BF16)

        k1 = g1.shape[1]
        xs = _out_proj(g1, g2, w_out[:k1], w_out[k1:], xs.reshape(m, d), mod, 2, segs).reshape(nb, t_tot, d)

        n_rows = t_lat if last else t_tot
        h2 = _norm_mod(xs, norm2_g[l], mod, 3, n_rows, t_lat).reshape(nb * n_rows, d)
        pq = _mm(h2, p_w_q[l].astype(BF16), F32)
        stats = _peer_topk(pq, p_subkeys[l])
        y = _peer_main(h2, p_u[l].astype(BF16), p_v[l].astype(BF16), stats, nkeys)
        out = _resid(xs, y.reshape(nb, n_rows, d), mod, 5, t_lat)
        xs = out
    return out
```
